```python
import math
import jax
import jax.numpy as jnp
from jax import lax
import numpy as np

D_MODEL = 4096
BATCH = 4
SEQ = 4096
DEPTH = 2

CTX_LEN = 256
GRID_W = 64
NORM_EPS = 1e-6
N_MOD = 6

ATT_HEADS = 16
ATT_KV_HEADS = 4
GQA_GROUP = ATT_HEADS // ATT_KV_HEADS
HEAD_DIM = 128
WINDOW = 128
ATT_BLOCK = 128
ROPE_BASE = 10000.0
ROPE_PAIRS_PER_AXIS = HEAD_DIM // 4
ATT_Q = ATT_HEADS * HEAD_DIM
ATT_KV = ATT_KV_HEADS * HEAD_DIM

SSD_HEADS = 32
SSD_HEAD_DIM = 64
SSD_GROUPS = 4
SSD_STATE = 128
SSD_CONV = 5
SSD_CHUNK = 128
SSD_INNER = SSD_HEADS * SSD_HEAD_DIM
SSD_BC = SSD_GROUPS * SSD_STATE
SSD_XBC = SSD_INNER + 2 * SSD_BC

ML_HEADS = 8
ML_QK_DIM = 128
ML_V_DIM = 256
ML_CHUNK = 64
ML_QK = ML_HEADS * ML_QK_DIM
ML_V = ML_HEADS * ML_V_DIM

N_EXPERTS = 32
N_EXPERT_GROUPS = 4
EXPERTS_PER_GROUP = N_EXPERTS // N_EXPERT_GROUPS
TOP_K = 2
EXPERT_FF = 512
MOE_BLOCK = 128

IN_LAYOUT = (
    ('att_q', ATT_Q),
    ('att_k', ATT_KV),
    ('att_v', ATT_KV),
    ('ssd_z', SSD_INNER),
    ('ssd_xbc', SSD_XBC),
    ('ssd_dt', 2 * SSD_HEADS),
    ('ml_q', ML_QK),
    ('ml_k', ML_QK),
    ('ml_v', ML_V),
    ('ml_o', ML_V),
    ('ml_i', 2 * ML_HEADS),
    ('ml_f', 2 * ML_HEADS),
    ('gate_att', D_MODEL),
    ('gate_ssd', D_MODEL),
    ('gate_ml', D_MODEL),
)
IN_WIDTH = sum(w for _, w in IN_LAYOUT)

kernel_name = 'hybrid_gated_diffusion_trunk'


def rms_norm(x, gain):
    xf = x.astype(jnp.float32)
    y = xf * lax.rsqrt(jnp.mean(xf * xf, axis=-1, keepdims=True) + NORM_EPS)
    return (y * gain.astype(jnp.float32)).astype(x.dtype)


def modulation(cond, w_mod, b_mod):
    m = jnp.dot(jax.nn.silu(cond), w_mod) + b_mod
    return jnp.split(m[..., None, :], N_MOD, axis=-1)


def modulate(h, shift, scale):
    return h * (1 + scale) + shift


def split_cols(u):
    parts = {}
    off = 0
    for name, w in IN_LAYOUT:
        parts[name] = u[..., off:off + w]
        off += w
    return parts


def split_heads(t, n_heads):
    return t.reshape(t.shape[0], t.shape[1], n_heads, -1)


def flip_seq(t):
    return jnp.flip(t, axis=1)


def axial_rope(rows):
    f32 = jnp.float32
    freqs = ROPE_BASE ** (-jnp.arange(ROPE_PAIRS_PER_AXIS, dtype=f32) / ROPE_PAIRS_PER_AXIS)
    shape = (rows, GRID_W, ROPE_PAIRS_PER_AXIS)
    ang_r = jnp.broadcast_to(jnp.arange(rows, dtype=f32)[:, None, None] * freqs, shape)
    ang_c = jnp.broadcast_to(jnp.arange(GRID_W, dtype=f32)[None, :, None] * freqs, shape)
    ang = jnp.concatenate([ang_r, ang_c], axis=-1).reshape(rows * GRID_W, 2 * ROPE_PAIRS_PER_AXIS)
    return jnp.cos(ang), jnp.sin(ang)


def apply_rope(t, cos, sin):
    half = HEAD_DIM // 2
    co = cos[:, None, :].astype(t.dtype)
    si = sin[:, None, :].astype(t.dtype)
    t1, t2 = t[..., :half], t[..., half:]
    return jnp.concatenate([t1 * co - t2 * si, t1 * si + t2 * co], axis=-1)


def window_attention(q, k, v, k_ctx, v_ctx, sink):
    bsz, n = q.shape[:2]
    nblk = n // ATT_BLOCK
    f32 = jnp.float32
    qb = q.reshape(bsz, nblk, ATT_BLOCK, ATT_KV_HEADS, GQA_GROUP, HEAD_DIM)
    pad = ((0, 0), (ATT_BLOCK, ATT_BLOCK), (0, 0), (0, 0))

    def band(t):
        tp = jnp.pad(t, pad).reshape(bsz, nblk + 2, ATT_BLOCK, ATT_KV_HEADS, HEAD_DIM)
        return jnp.concatenate([tp[:, :-2], tp[:, 1:-1], tp[:, 2:]], axis=2)

    kw, vw = band(k), band(v)
    s_win = jnp.einsum('bnqhgd,bnkhd->bnhgqk', qb, kw, preferred_element_type=f32)
    s_ctx = jnp.einsum('bnqhgd,bchd->bnhgqc', qb, k_ctx, preferred_element_type=f32)
    qpos = jnp.arange(nblk)[:, None, None] * ATT_BLOCK + jnp.arange(ATT_BLOCK)[None, :, None]
    kpos = (jnp.arange(nblk)[:, None, None] - 1) * ATT_BLOCK + jnp.arange(3 * ATT_BLOCK)[None, None, :]
    valid = (jnp.abs(kpos - qpos) <= WINDOW) & (kpos >= 0) & (kpos < n)
    s_win = jnp.where(valid[:, None, None], s_win, -jnp.inf)
    sink_l = jnp.broadcast_to(sink.astype(f32).reshape(ATT_KV_HEADS, GQA_GROUP)[:, :, None, None], s_win.shape[:-1] + (1,))
    p = jax.nn.softmax(jnp.concatenate([sink_l, s_ctx, s_win], axis=-1), axis=-1)
    n_ctx = k_ctx.shape[1]
    p_ctx = p[..., 1:1 + n_ctx].astype(v.dtype)
    p_win = p[..., 1 + n_ctx:].astype(v.dtype)
    o = jnp.einsum('bnhgqc,bchd->bnqhgd', p_ctx, v_ctx) + jnp.einsum('bnhgqk,bnkhd->bnqhgd', p_win, vw)
    return o.reshape(bsz, n, ATT_Q)


def context_attention(q, k, v, sink):
    bsz, n = q.shape[:2]
    f32 = jnp.float32
    qg = q.reshape(bsz, n, ATT_KV_HEADS, GQA_GROUP, HEAD_DIM)
    s = jnp.einsum('bqhgd,bkhd->bhgqk', qg, k, preferred_element_type=f32)
    sink_l = jnp.broadcast_to(sink.astype(f32).reshape(ATT_KV_HEADS, GQA_GROUP)[:, :, None, None], s.shape[:-1] + (1,))
    p = jax.nn.softmax(jnp.concatenate([sink_l, s], axis=-1), axis=-1)[..., 1:].astype(v.dtype)
    o = jnp.einsum('bhgqk,bkhd->bqhgd', p, v)
    return o.reshape(bsz, n, ATT_Q)


def centred_dwconv(x, w, b):
    pad = SSD_CONV // 2
    y = lax.conv_general_dilated(x, w[:, None, :].astype(x.dtype), window_strides=(1,), padding=((pad, pad),),
                                 dimension_numbers=('NWC', 'WIO', 'NWC'), feature_group_count=x.shape[-1])
    return y + b.astype(x.dtype)


def ssd_chunked(x, dt, a_head, bm, cm, h0, with_out):
    f32 = jnp.float32
    bsz, n = x.shape[:2]
    g, r = SSD_GROUPS, SSD_HEADS // SSD_GROUPS
    q = SSD_CHUNK
    nc = n // q
    la = (dt * a_head).reshape(bsz, nc, q, g, r)
    xdt = (x.astype(f32) * dt[..., None]).reshape(bsz, nc, q, g, r, SSD_HEAD_DIM)
    bc = bm.astype(f32).reshape(bsz, nc, q, g, SSD_STATE)
    cc = cm.astype(f32).reshape(bsz, nc, q, g, SSD_STATE)
    acum = jnp.cumsum(la, axis=2)
    a_tot = acum[:, :, -1]
    decay_end = jnp.exp(a_tot[:, :, None] - acum)
    states = jnp.einsum('bcsgn,bcsgr,bcsgrp->bcgrpn', bc, decay_end, xdt)

    def step(h, inp):
        s_c, a_c = inp
        return jnp.exp(a_c)[..., None, None] * h + s_c, h

    h_last, h_in = lax.scan(step, h0.reshape(bsz, g, r, SSD_HEAD_DIM, SSD_STATE).astype(f32),
                            (jnp.moveaxis(states, 1, 0), jnp.moveaxis(a_tot, 1, 0)))
    if not with_out:
        return None, h_last
    h_in = jnp.moveaxis(h_in, 0, 1)
    tri = jnp.tril(jnp.ones((q, q), dtype=bool))
    seg = acum[:, :, :, None] - acum[:, :, None, :]
    lmat = jnp.exp(jnp.where(tri[None, None, :, :, None, None], seg, -jnp.inf))
    cb = jnp.einsum('bctgn,bcsgn->bctsg', cc, bc)
    y_diag = jnp.einsum('bctsg,bctsgr,bcsgrp->bctgrp', cb, lmat, xdt)
    y_off = jnp.einsum('bctgn,bcgrpn,bctgr->bctgrp', cc, h_in, jnp.exp(acum))
    return (y_diag + y_off).reshape(bsz, n, SSD_HEADS, SSD_HEAD_DIM), h_last


def ssd_branch(z, xbc, dt_raw, conv_w, conv_b, dt_bias, a_log, d_skip, norm_g, init, with_out):
    bsz, n = z.shape[:2]
    f32 = jnp.float32
    xbc = jax.nn.silu(centred_dwconv(xbc, conv_w, conv_b))
    xs = xbc[..., :SSD_INNER].reshape(bsz, n, SSD_HEADS, SSD_HEAD_DIM)
    bm = xbc[..., SSD_INNER:SSD_INNER + SSD_BC].reshape(bsz, n, SSD_GROUPS, SSD_STATE)
    cm = xbc[..., SSD_INNER + SSD_BC:].reshape(bsz, n, SSD_GROUPS, SSD_STATE)
    dt = jax.nn.softplus(dt_raw.astype(f32).reshape(bsz, n, 2, SSD_HEADS) + dt_bias.astype(f32))
    a_head = -jnp.exp(a_log.astype(f32))
    y_f, h_f = ssd_chunked(xs, dt[:, :, 0], a_head[0], bm, cm, init[0], with_out)
    y_b, h_b = ssd_chunked(flip_seq(xs), flip_seq(dt[:, :, 1]), a_head[1], flip_seq(bm), flip_seq(cm), init[1], with_out)
    if not with_out:
        return None, (h_f, h_b)
    y = y_f + flip_seq(y_b) + xs.astype(f32) * d_skip.astype(f32)[:, None]
    y = y.reshape(bsz, n, SSD_INNER) * jax.nn.silu(z.astype(f32))
    return rms_norm(y, norm_g).astype(z.dtype), (h_f, h_b)


def mlstm_chunked(q, k, v, log_i, log_f, init, with_out):
    bsz, n, nh, dk = q.shape
    dv = v.shape[-1]
    qn = ML_CHUNK
    nc = n // qn
    qc = q.reshape(bsz, nc, qn, nh, dk)
    kc = k.reshape(bsz, nc, qn, nh, dk)
    vc = v.reshape(bsz, nc, qn, nh, dv)
    li = log_i.reshape(bsz, nc, qn, nh)
    bcum = jnp.cumsum(log_f.reshape(bsz, nc, qn, nh), axis=2)
    g_tot = bcum[:, :, -1]
    w_end = g_tot[:, :, None] - bcum + li
    m_loc = jnp.max(w_end, axis=2)
    e_end = jnp.exp(w_end - m_loc[:, :, None])
    s_loc = jnp.einsum('bcsh,bcshk,bcshv->bchkv', e_end, kc, vc)
    n_loc = jnp.einsum('bcsh,bcshk->bchk', e_end, kc)

    def step(carry, inp):
        s_prev, n_prev, m_prev = carry
        s_c, n_c, m_c, g_c = inp
        m_new = jnp.maximum(g_c + m_prev, m_c)
        wa = jnp.exp(g_c + m_prev - m_new)
        wb = jnp.exp(m_c - m_new)
        new = (wa[..., None, None] * s_prev + wb[..., None, None] * s_c, wa[..., None] * n_prev + wb[..., None] * n_c, m_new)
        return new, (s_prev, n_prev, m_prev)

    seq_in = (jnp.moveaxis(s_loc, 1, 0), jnp.moveaxis(n_loc, 1, 0), jnp.moveaxis(m_loc, 1, 0), jnp.moveaxis(g_tot, 1, 0))
    final, (s_in, n_in, m_in) = lax.scan(step, init, seq_in)
    if not with_out:
        return None, final
    s_in = jnp.moveaxis(s_in, 0, 1)
    n_in = jnp.moveaxis(n_in, 0, 1)
    m_in = jnp.moveaxis(m_in, 0, 1)
    tri = jnp.tril(jnp.ones((qn, qn), dtype=bool))
    dmat = bcum[:, :, :, None] - bcum[:, :, None, :] + li[:, :, None, :]
    dmat = jnp.where(tri[None, None, :, :, None], dmat, -jnp.inf)
    inter = bcum + m_in[:, :, None]
    m_t = jnp.maximum(inter, jnp.max(dmat, axis=3))
    pmat = jnp.exp(dmat - m_t[:, :, :, None]) * jnp.einsum('bcthk,bcshk->bctsh', qc, kc)
    e_int = jnp.exp(inter - m_t)
    num = jnp.einsum('bctsh,bcshv->bcthv', pmat, vc) + e_int[..., None] * jnp.einsum('bcthk,bchkv->bcthv', qc, s_in)
    den = jnp.sum(pmat, axis=3) + e_int * jnp.einsum('bcthk,bchk->bcth', qc, n_in)
    h = num / jnp.maximum(jnp.abs(den), jnp.exp(-m_t))[..., None]
    return h.reshape(bsz, n, nh, dv), final


def mlstm_branch(uq, uk, uv, uo, ui, uf, i_bias, f_bias, norm_g, init, with_out):
    bsz, n = uq.shape[:2]
    f32 = jnp.float32
    q = uq.astype(f32).reshape(bsz, n, ML_HEADS, ML_QK_DIM)
    k = uk.astype(f32).reshape(bsz, n, ML_HEADS, ML_QK_DIM) * ML_QK_DIM ** -0.5
    v = uv.astype(f32).reshape(bsz, n, ML_HEADS, ML_V_DIM)
    log_i = ui.astype(f32).reshape(bsz, n, 2, ML_HEADS) + i_bias.astype(f32)
    log_f = jax.nn.log_sigmoid(uf.astype(f32).reshape(bsz, n, 2, ML_HEADS) + f_bias.astype(f32))
    h_f, st_f = mlstm_chunked(q, k, v, log_i[:, :, 0], log_f[:, :, 0], init[0], with_out)
    h_b, st_b = mlstm_chunked(flip_seq(q), flip_seq(k), flip_seq(v), flip_seq(log_i[:, :, 1]), flip_seq(log_f[:, :, 1]), init[1], with_out)
    if not with_out:
        return None, (st_f, st_b)
    h = rms_norm(h_f + flip_seq(h_b), norm_g.reshape(ML_HEADS, ML_V_DIM))
    h = h.reshape(bsz, n, ML_V) * jax.nn.sigmoid(uo.astype(f32))
    return h.astype(uq.dtype), (st_f, st_b)


def merge_branches(p, y_att, y_ssd, y_ml, w_br_att, w_br_ssd, w_br_ml, w_out):
    m = (jax.nn.sigmoid(p['gate_att']) * jnp.dot(y_att, w_br_att)
         + jax.nn.sigmoid(p['gate_ssd']) * jnp.dot(y_ssd, w_br_ssd)
         + jax.nn.sigmoid(p['gate_ml']) * jnp.dot(y_ml, w_br_ml))
    return jnp.dot(m, w_out)


def hybrid_mixer(ax, ac, cos, sin, w_in, q_g, k_g, sink, conv_w, conv_b, dt_bias, a_log, d_skip, ssd_g,
                 i_bias, f_bias, ml_g, w_br_att, w_br_ssd, w_br_ml, w_out, ctx_out):
    bsz = ax.shape[0]
    f32 = jnp.float32
    att_scale = HEAD_DIM ** -0.5
    px = split_cols(jnp.dot(ax, w_in))
    pc = split_cols(jnp.dot(ac, w_in))
    kc = rms_norm(split_heads(pc['att_k'], ATT_KV_HEADS), k_g)
    vc = split_heads(pc['att_v'], ATT_KV_HEADS)
    qx = apply_rope(rms_norm(split_heads(px['att_q'], ATT_HEADS), q_g), cos, sin) * att_scale
    kx = apply_rope(rms_norm(split_heads(px['att_k'], ATT_KV_HEADS), k_g), cos, sin)
    vx = split_heads(px['att_v'], ATT_KV_HEADS)
    att_x = window_attention(qx, kx, vx, kc, vc, sink)
    zero_h = jnp.zeros((bsz, SSD_HEADS, SSD_HEAD_DIM, SSD_STATE), f32)
    ssd_c, ssd_state = ssd_branch(pc['ssd_z'], pc['ssd_xbc'], pc['ssd_dt'], conv_w, conv_b, dt_bias, a_log, d_skip, ssd_g,
                                  (zero_h, zero_h), ctx_out)
    ssd_x, _ = ssd_branch(px['ssd_z'], px['ssd_xbc'], px['ssd_dt'], conv_w, conv_b, dt_bias, a_log, d_skip, ssd_g,
                          ssd_state, True)
    zero_ml = (jnp.zeros((bsz, ML_HEADS, ML_QK_DIM, ML_V_DIM), f32), jnp.zeros((bsz, ML_HEADS, ML_QK_DIM), f32),
               jnp.zeros((bsz, ML_HEADS), f32))
    ml_c, ml_state = mlstm_branch(pc['ml_q'], pc['ml_k'], pc['ml_v'], pc['ml_o'], pc['ml_i'], pc['ml_f'], i_bias, f_bias, ml_g,
                                  (zero_ml, zero_ml), ctx_out)
    ml_x, _ = mlstm_branch(px['ml_q'], px['ml_k'], px['ml_v'], px['ml_o'], px['ml_i'], px['ml_f'], i_bias, f_bias, ml_g,
                           ml_state, True)
    y_x = merge_branches(px, att_x, ssd_x, ml_x, w_br_att, w_br_ssd, w_br_ml, w_out)
    if not ctx_out:
        return y_x, None
    qc = rms_norm(split_heads(pc['att_q'], ATT_HEADS), q_g) * att_scale
    att_c = context_attention(qc, kc, vc, sink)
    y_c = merge_branches(pc, att_c, ssd_c, ml_c, w_br_att, w_br_ssd, w_br_ml, w_out)
    return y_x, y_c


def route(h, router_w, router_b):
    f32 = jnp.float32
    s = jax.nn.sigmoid(jnp.dot(h.astype(f32), router_w.astype(f32)))
    sel = (s + router_b.astype(f32)).reshape(-1, N_EXPERT_GROUPS, EXPERTS_PER_GROUP)
    grp = jnp.sum(lax.top_k(sel, TOP_K)[0], axis=-1)
    g_best = jnp.argmax(grp, axis=-1).astype(jnp.int32)
    sel_g = jnp.take_along_axis(sel, g_best[:, None, None], axis=1)[:, 0]
    _, e_loc = lax.top_k(sel_g, TOP_K)
    e_idx = g_best[:, None] * EXPERTS_PER_GROUP + e_loc.astype(jnp.int32)
    w = jnp.take_along_axis(s, e_idx, axis=1)
    return e_idx, w / jnp.sum(w, axis=-1, keepdims=True)


def moe_ffn(h, router_w, router_b, w_gate, w_up, w_down):
    n_tok, d = h.shape
    e_idx, gate = route(h, router_w, router_b)
    flat_e = e_idx.reshape(-1)
    n_assign = n_tok * TOP_K
    order = jnp.argsort(flat_e)
    sorted_e = flat_e[order]
    tok = order // TOP_K
    sizes = jnp.bincount(flat_e, length=N_EXPERTS)
    padded = (sizes + MOE_BLOCK - 1) // MOE_BLOCK * MOE_BLOCK
    pad_end = jnp.cumsum(padded)
    pad_start = pad_end - padded
    start = jnp.cumsum(sizes) - sizes
    dest = pad_start[sorted_e] + jnp.arange(n_assign) - start[sorted_e]
    n_blocks = (n_assign + N_EXPERTS * (MOE_BLOCK - 1) + MOE_BLOCK - 1) // MOE_BLOCK
    buf = jnp.zeros((n_blocks * MOE_BLOCK, d), h.dtype).at[dest].set(h[tok])
    block_e = jnp.minimum(jnp.searchsorted(pad_end, jnp.arange(n_blocks) * MOE_BLOCK, side='right'), N_EXPERTS - 1)

    def expert_block(args):
        xb, e = args
        return (jax.nn.silu(jnp.dot(xb, w_gate[e])) * jnp.dot(xb, w_up[e])) @ w_down[e]

    yb = lax.map(expert_block, (buf.reshape(n_blocks, MOE_BLOCK, d), block_e))
    ys = yb.reshape(n_blocks * MOE_BLOCK, d)[dest] * gate.reshape(-1)[order][:, None].astype(h.dtype)
    return jnp.zeros_like(h).at[tok].add(ys)


def setup_inputs(seed: int = 0) -> dict:
    key = jax.random.key(seed)
    ks = jax.random.split(key, 32)
    f32 = jnp.float32
    dm = D_MODEL

    def nrm(k, shape, scale):
        return jax.random.normal(k, shape, f32) * scale

    dt0 = jnp.exp(jax.random.uniform(ks[14], (DEPTH, 2, SSD_HEADS), f32, math.log(1e-3), math.log(1e-1)))
    return {
        'x': nrm(ks[0], (BATCH, SEQ, dm), 1.0),
        'c': nrm(ks[1], (BATCH, dm), 1.0),
        'ctx': nrm(ks[2], (BATCH, CTX_LEN, dm), 1.0),
        'c_ctx': nrm(ks[3], (dm,), 1.0),
        'norm1_g': 1.0 + nrm(ks[4], (DEPTH, dm), 0.02),
        'norm2_g': 1.0 + nrm(ks[5], (DEPTH, dm), 0.02),
        'w_mod': nrm(ks[6], (DEPTH, dm, N_MOD * dm), 0.5 * dm ** -0.5),
        'b_mod': nrm(ks[7], (DEPTH, N_MOD * dm), 0.02),
        'w_in': nrm(ks[8], (DEPTH, dm, IN_WIDTH), dm ** -0.5),
        'q_norm_g': 1.0 + nrm(ks[9], (DEPTH, HEAD_DIM), 0.02),
        'k_norm_g': 1.0 + nrm(ks[10], (DEPTH, HEAD_DIM), 0.02),
        'attn_sink': nrm(ks[11], (DEPTH, ATT_HEADS), 1.0),
        'ssd_conv_w': nrm(ks[12], (DEPTH, SSD_CONV, SSD_XBC), SSD_CONV ** -0.5),
        'ssd_conv_b': nrm(ks[13], (DEPTH, SSD_XBC), 0.02),
        'ssd_dt_bias': dt0 + jnp.log(-jnp.expm1(-dt0)),
        'ssd_a_log': jnp.log(jax.random.uniform(ks[15], (DEPTH, 2, SSD_HEADS), f32, 1.0, 16.0)),
        'ssd_d': 1.0 + nrm(ks[16], (DEPTH, SSD_HEADS), 0.02),
        'ssd_norm_g': 1.0 + nrm(ks[17], (DEPTH, SSD_INNER), 0.02),
        'ml_i_bias': nrm(ks[18], (DEPTH, 2, ML_HEADS), 0.1),
        'ml_f_bias': jax.random.uniform(ks[19], (DEPTH, 2, ML_HEADS), f32, 3.0, 6.0),
        'ml_norm_g': 1.0 + nrm(ks[20], (DEPTH, ML_V), 0.02),
        'w_br_att': nrm(ks[21], (DEPTH, ATT_Q, dm), ATT_Q ** -0.5),
        'w_br_ssd': nrm(ks[22], (DEPTH, SSD_INNER, dm), SSD_INNER ** -0.5),
        'w_br_ml': nrm(ks[23], (DEPTH, ML_V, dm), ML_V ** -0.5),
        'w_out': nrm(ks[24], (DEPTH, dm, dm), dm ** -0.5),
        'router_w': nrm(ks[25], (dm, N_EXPERTS), dm ** -0.5),
        'router_b': nrm(ks[26], (N_EXPERTS,), 0.01),
        'w_gate': nrm(ks[27], (DEPTH, N_EXPERTS, dm, EXPERT_FF), dm ** -0.5),
        'w_up': nrm(ks[28], (DEPTH, N_EXPERTS, dm, EXPERT_FF), dm ** -0.5),
        'w_down': nrm(ks[29], (DEPTH, N_EXPERTS, EXPERT_FF, dm), EXPERT_FF ** -0.5),
    }


def reference(x, c, ctx, c_ctx, norm1_g, norm2_g, w_mod, b_mod, w_in, q_norm_g, k_norm_g, attn_sink,
              ssd_conv_w, ssd_conv_b, ssd_dt_bias, ssd_a_log, ssd_d, ssd_norm_g,
              ml_i_bias, ml_f_bias, ml_norm_g, w_br_att, w_br_ssd, w_br_ml, w_out,
              router_w, router_b, w_gate, w_up, w_down):
    n_lat = x.shape[1]
    rows = n_lat // GRID_W
    cos, sin = axial_rope(rows)
    for l in range(DEPTH):
        ctx_out = l < DEPTH - 1
        mx = modulation(c, w_mod[l], b_mod[l])
        mc = modulation(c_ctx, w_mod[l], b_mod[l])
        ax = modulate(rms_norm(x, norm1_g[l]), mx[0], mx[1])
        ac = modulate(rms_norm(ctx, norm1_g[l]), mc[0], mc[1])
        yx, yc = hybrid_mixer(ax, ac, cos, sin, w_in[l], q_norm_g[l], k_norm_g[l], attn_sink[l],
                              ssd_conv_w[l], ssd_conv_b[l], ssd_dt_bias[l], ssd_a_log[l], ssd_d[l], ssd_norm_g[l],
                              ml_i_bias[l], ml_f_bias[l], ml_norm_g[l], w_br_att[l], w_br_ssd[l], w_br_ml[l], w_out[l],
                              ctx_out)
        x = x + mx[2] * yx
        bx = modulate(rms_norm(x, norm2_g[l]), mx[3], mx[4]).reshape(-1, D_MODEL)
        if ctx_out:
            ctx = ctx + mc[2] * yc
            bc = modulate(rms_norm(ctx, norm2_g[l]), mc[3], mc[4]).reshape(-1, D_MODEL)
            y = moe_ffn(jnp.concatenate([bx, bc], axis=0), router_w, router_b, w_gate[l], w_up[l], w_down[l])
            x = x + mx[5] * y[:bx.shape[0]].reshape(x.shape)
            ctx = ctx + mc[5] * y[bx.shape[0]:].reshape(ctx.shape)
        else:
            y = moe_ffn(bx, router_w, router_b, w_gate[l], w_up[l], w_down[l])
            x = x + mx[5] * y.reshape(x.shape)
    return x
```

```python
import functools
import math

import jax
import jax.numpy as jnp
from jax import lax
from jax.experimental import pallas as pl
from jax.experimental.pallas import tpu as pltpu

D_MODEL = 4096
BATCH = 4
SEQ = 4096
DEPTH = 2
CTX_LEN = 256
GRID_W = 64
NORM_EPS = 1e-6
N_MOD = 6

ATT_HEADS = 16
ATT_KV_HEADS = 4
GQA_GROUP = ATT_HEADS // ATT_KV_HEADS
HEAD_DIM = 128
WINDOW = 128
ROPE_BASE = 10000.0
ROPE_PAIRS_PER_AXIS = HEAD_DIM // 4
ATT_Q = ATT_HEADS * HEAD_DIM
ATT_KV = ATT_KV_HEADS * HEAD_DIM

SSD_HEADS = 32
SSD_HEAD_DIM = 64
SSD_GROUPS = 4
SSD_STATE = 128
SSD_CONV = 5
SSD_INNER = SSD_HEADS * SSD_HEAD_DIM
SSD_BC = SSD_GROUPS * SSD_STATE
SSD_XBC = SSD_INNER + 2 * SSD_BC

ML_HEADS = 8
ML_QK_DIM = 128
ML_V_DIM = 256
ML_QK = ML_HEADS * ML_QK_DIM
ML_V = ML_HEADS * ML_V_DIM

N_EXPERTS = 32
N_EXPERT_GROUPS = 4
EXPERTS_PER_GROUP = N_EXPERTS // N_EXPERT_GROUPS
TOP_K = 2
EXPERT_FF = 512

LANES = 128
VMEM_LIMIT_BYTES = 56 * 1024 * 1024

CHUNK = 128
MOE_ROWS = 256
NEG_BIG = -1e30

f32 = jnp.float32
bf16 = jnp.bfloat16

_OFF_DT = ATT_Q + 2 * ATT_KV + SSD_INNER + SSD_XBC
_OFF_MLQ = _OFF_DT + 2 * SSD_HEADS
_OFF_MLI = _OFF_MLQ + 2 * ML_QK + 2 * ML_V
_OFF_GATE = _OFF_MLI + 4 * ML_HEADS
C_Q = 0
C_K = ATT_Q
C_V = ATT_Q + ATT_KV
C_Z = ATT_Q + 2 * ATT_KV
C_XBC = C_Z + SSD_INNER
C_MLQ = C_XBC + SSD_XBC
C_MLK = C_MLQ + ML_QK
C_MLV = C_MLK + ML_QK
C_MLO = C_MLV + ML_V
C_GATE = C_MLO + ML_V
S_DT = 0
S_MLI = 2 * SSD_HEADS
S_MLF = S_MLI + 2 * ML_HEADS


def _params(*sem):
    return pltpu.CompilerParams(dimension_semantics=sem, vmem_limit_bytes=VMEM_LIMIT_BYTES)


def _pick(n, cands):
    for c in cands:
        if n % c == 0:
            return c
    raise ValueError((n, cands))


def _n_tok():
    return BATCH * (SEQ + CTX_LEN)


def _mod_row(row_start):
    return jnp.minimum(row_start // SEQ, BATCH)


def _split3(v):
    h1 = v.astype(bf16)
    r1 = v - h1.astype(f32)
    h2 = r1.astype(bf16)
    h3 = (r1 - h2.astype(f32)).astype(bf16)
    return h1, h2, h3


def _dot(a, b):
    return jnp.dot(a, b, preferred_element_type=f32)


def _dot_nt(a, b):
    return lax.dot_general(a, b, (((1,), (1,)), ((), ())), preferred_element_type=f32)


def _dot_tn(a, b):
    return lax.dot_general(a, b, (((0,), (0,)), ((), ())), preferred_element_type=f32)


def _dot_exact(v, m01):
    h1, h2, h3 = _split3(v)
    return _dot(h1, m01) + _dot(h2, m01) + _dot(h3, m01)


def _softplus(v):
    return jnp.maximum(v, 0.0) + jnp.log1p(jnp.exp(-jnp.abs(v)))


def _sigmoid(v):
    return 1.0 / (1.0 + jnp.exp(-v))


def _silu(v):
    return v * _sigmoid(v)


def _mod_kernel(c_ref, w_ref, b_ref, o_ref):
    s = _silu(c_ref[...]).astype(bf16)
    o_ref[...] = _dot(s, w_ref[...].astype(bf16)) + b_ref[...]


def _modulation(cond8, w_mod, b_mod):
    n = N_MOD * D_MODEL
    tn = _pick(n, (512, 256, 128))
    return pl.pallas_call(
        _mod_kernel,
        grid=(n // tn,),
        in_specs=[pl.BlockSpec((8, D_MODEL), lambda j: (0, 0)),
                  pl.BlockSpec((D_MODEL, tn), lambda j: (0, j)),
                  pl.BlockSpec((1, tn), lambda j: (0, j))],
        out_specs=pl.BlockSpec((8, tn), lambda j: (0, j)),
        out_shape=jax.ShapeDtypeStruct((8, n), f32),
        compiler_params=_params("parallel"),
        name="modulation",
    )(cond8, w_mod, b_mod.reshape(1, n))


def _norm_mod_kernel(x_ref, g_ref, sh_ref, sc_ref, *rest, router):
    x = x_ref[...]
    y = x * lax.rsqrt(jnp.mean(x * x, axis=-1, keepdims=True) + NORM_EPS) * g_ref[...]
    h = y * (1.0 + sc_ref[...]) + sh_ref[...]
    if not router:
        rest[0][...] = h.astype(bf16)
        return
    rw_ref, o_ref, lg_ref = rest
    h1 = h.astype(bf16)
    h2 = (h - h1.astype(f32)).astype(bf16)
    o_ref[...] = h1
    rw = rw_ref[...]
    w1 = rw.astype(bf16)
    w2 = (rw - w1.astype(f32)).astype(bf16)
    lg_ref[...] = _dot_nt(w1, h1) + _dot_nt(w1, h2) + _dot_nt(w2, h1)


def _norm_mod(tokens, n_rows, gain, mod6, k_shift, k_scale, router_wt=None):
    tm = _pick(SEQ, (256, 128))
    router = router_wt is not None
    in_specs = [pl.BlockSpec((tm, D_MODEL), lambda i: (i, 0)),
                pl.BlockSpec((1, D_MODEL), lambda i: (0, 0)),
                pl.BlockSpec((None, None, 1, D_MODEL), lambda i: (_mod_row(i * tm), k_shift, 0, 0)),
                pl.BlockSpec((None, None, 1, D_MODEL), lambda i: (_mod_row(i * tm), k_scale, 0, 0))]
    args = [tokens, gain.reshape(1, D_MODEL), mod6, mod6]
    out_specs = pl.BlockSpec((tm, D_MODEL), lambda i: (i, 0))
    out_shape = jax.ShapeDtypeStruct((n_rows, D_MODEL), bf16)
    if router:
        in_specs.append(pl.BlockSpec((N_EXPERTS, D_MODEL), lambda i: (0, 0)))
        args.append(router_wt)
        out_specs = (out_specs, pl.BlockSpec((N_EXPERTS, tm), lambda i: (0, i)))
        out_shape = (out_shape, jax.ShapeDtypeStruct((N_EXPERTS, n_rows), f32))
    return pl.pallas_call(
        functools.partial(_norm_mod_kernel, router=router),
        grid=(n_rows // tm,),
        in_specs=in_specs,
        out_specs=out_specs,
        out_shape=out_shape,
        compiler_params=_params("parallel"),
        name="norm_mod_router" if router else "norm_mod",
    )(*args)


def _mm_kernel(a_ref, w_ref, o_ref):
    o_ref[...] = _dot(a_ref[...], w_ref[...]).astype(o_ref.dtype)


def _matmul(a, w, out_dtype, name):
    m, k = a.shape
    n = w.shape[1]
    tm = _pick(m, (1024, 512, 256, 128))
    tn = _pick(n, (512, 256, 128))
    return pl.pallas_call(
        _mm_kernel,
        grid=(m // tm, n // tn),
        in_specs=[pl.BlockSpec((tm, k), lambda i, j: (i, 0)),
                  pl.BlockSpec((k, tn), lambda i, j: (0, j))],
        out_specs=pl.BlockSpec((tm, tn), lambda i, j: (i, j)),
        out_shape=jax.ShapeDtypeStruct((m, n), out_dtype),
        compiler_params=_params("parallel", "parallel"),
        name=name,
    )(a, w)


def _rms_head(t, g):
    return t * lax.rsqrt(jnp.mean(t * t, axis=-1, keepdims=True) + NORM_EPS) * g


def _rope(t, cos, sin_signed):
    return t * cos + pltpu.roll(t, HEAD_DIM // 2, 1) * sin_signed


def _kprep_kernel(k_ref, cos_ref, sin_ref, g_ref, o_ref):
    cos = cos_ref[...]
    sin = sin_ref[...]
    g = g_ref[...]
    for h in range(ATT_KV_HEADS):
        sl = slice(h * HEAD_DIM, (h + 1) * HEAD_DIM)
        t = _rms_head(k_ref[:, sl].astype(f32), g)
        o_ref[:, sl] = _rope(t, cos, sin).astype(bf16)


def _rope_block(i):
    return jnp.where(i < BATCH * SEQ // CHUNK, i % (SEQ // CHUNK), SEQ // CHUNK)


def _kprep(u_big, cos_t, sin_t, k_gain):
    t_all = _n_tok()
    return pl.pallas_call(
        _kprep_kernel,
        grid=(t_all // CHUNK,),
        in_specs=[pl.BlockSpec((CHUNK, ATT_KV), lambda i: (i, C_K // ATT_KV)),
                  pl.BlockSpec((CHUNK, HEAD_DIM), lambda i: (_rope_block(i), 0)),
                  pl.BlockSpec((CHUNK, HEAD_DIM), lambda i: (_rope_block(i), 0)),
                  pl.BlockSpec((1, HEAD_DIM), lambda i: (0, 0))],
        out_specs=pl.BlockSpec((CHUNK, ATT_KV), lambda i: (i, 0)),
        out_shape=jax.ShapeDtypeStruct((t_all, ATT_KV), bf16),
        compiler_params=_params("parallel"),
        name="attn_kprep",
    )(u_big, cos_t, sin_t, k_gain.reshape(1, HEAD_DIM))


def _attn_kernel(sink_ref, q_ref, cos_ref, sin_ref, g_ref, *rest, window):
    if window:
        kx_ref, kp_ref, kc_ref, kn_ref, vx_ref, vp_ref, vc_ref, vn_ref, o_ref = rest
        k_all = jnp.concatenate([kx_ref[...], kp_ref[...], kc_ref[...], kn_ref[...]], axis=0)
        v_all = jnp.concatenate([vx_ref[...], vp_ref[...], vc_ref[...], vn_ref[...]], axis=0)
    else:
        kx_ref, vx_ref, o_ref = rest
        k_all = kx_ref[...]
        v_all = vx_ref[...]
    hkv = pl.program_id(1)
    n = pl.program_id(2)
    cos = cos_ref[...]
    sin = sin_ref[...]
    g = g_ref[...]
    scale = HEAD_DIM ** -0.5
    qs = []
    for j in range(GQA_GROUP):
        t = _rms_head(q_ref[:, j * HEAD_DIM:(j + 1) * HEAD_DIM].astype(f32), g)
        qs.append((_rope(t, cos, sin) * scale).astype(bf16))
    q_all = jnp.concatenate(qs, axis=0)
    rows = GQA_GROUP * CHUNK
    s = _dot_nt(q_all, k_all)
    n_keys = k_all.shape[0]
    row = lax.broadcasted_iota(jnp.int32, (rows, 1), 0)
    if window:
        col = lax.broadcasted_iota(jnp.int32, (rows, n_keys), 1)
        tq = lax.broadcasted_iota(jnp.int32, (rows, n_keys), 0) % CHUNK
        rel = col - CTX_LEN - CHUNK - tq
        kpos = (n - 1) * CHUNK + (col - CTX_LEN)
        valid = (col < CTX_LEN) | ((jnp.abs(rel) <= WINDOW) & (kpos >= 0) & (kpos < SEQ))
        s = jnp.where(valid, s, NEG_BIG)
    sink = jnp.zeros((rows, 1), f32)
    for j in range(GQA_GROUP):
        sink = jnp.where(row // CHUNK == j, sink_ref[hkv * GQA_GROUP + j], sink)
    m = jnp.maximum(jnp.max(s, axis=-1, keepdims=True), sink)
    p = jnp.exp(s - m)
    denom = jnp.sum(p, axis=-1, keepdims=True) + jnp.exp(sink - m)
    o = _dot(p.astype(bf16), v_all) / denom
    for j in range(GQA_GROUP):
        o_ref[:, j * HEAD_DIM:(j + 1) * HEAD_DIM] = o[j * CHUNK:(j + 1) * CHUNK].astype(bf16)


def _attention(u_big, k_rot, cos_t, sin_t, q_gain, sink, window):
    qw = GQA_GROUP * HEAD_DIM
    nlat = BATCH * SEQ // CHUNK
    cpb = CTX_LEN // CHUNK
    vcol = C_V // HEAD_DIM
    smem = pl.BlockSpec(memory_space=pltpu.SMEM)
    gain = pl.BlockSpec((1, HEAD_DIM), lambda b, h, n: (0, 0))
    ctx_k = pl.BlockSpec((CTX_LEN, HEAD_DIM), lambda b, h, n: (nlat // cpb + b, h))
    ctx_v = pl.BlockSpec((CTX_LEN, HEAD_DIM), lambda b, h, n: (nlat // cpb + b, vcol + h))
    if window:
        nb = SEQ // CHUNK
        rb = lambda b, n: b * nb + n
        cl = lambda n: jnp.clip(n, 0, nb - 1)
        q_spec = pl.BlockSpec((CHUNK, qw), lambda b, h, n: (rb(b, n), h))
        tab = pl.BlockSpec((CHUNK, HEAD_DIM), lambda b, h, n: (n, 0))
        kspec = lambda d: pl.BlockSpec((CHUNK, HEAD_DIM), lambda b, h, n: (rb(b, cl(n + d)), h))
        vspec = lambda d: pl.BlockSpec((CHUNK, HEAD_DIM), lambda b, h, n: (rb(b, cl(n + d)), vcol + h))
        in_specs = [smem, q_spec, tab, tab, gain, ctx_k, kspec(-1), kspec(0), kspec(1),
                    ctx_v, vspec(-1), vspec(0), vspec(1)]
        args = (sink, u_big, cos_t, sin_t, q_gain.reshape(1, HEAD_DIM), k_rot, k_rot, k_rot, k_rot,
                u_big, u_big, u_big, u_big)
        grid = (BATCH, ATT_KV_HEADS, nb)
        out_rows = BATCH * SEQ
        out_spec = pl.BlockSpec((CHUNK, qw), lambda b, h, n: (rb(b, n), h))
    else:
        rb = lambda b, n: nlat + b * cpb + n
        q_spec = pl.BlockSpec((CHUNK, qw), lambda b, h, n: (rb(b, n), h))
        tab = pl.BlockSpec((CHUNK, HEAD_DIM), lambda b, h, n: (SEQ // CHUNK, 0))
        in_specs = [smem, q_spec, tab, tab, gain, ctx_k, ctx_v]
        args = (sink, u_big, cos_t, sin_t, q_gain.reshape(1, HEAD_DIM), k_rot, u_big)
        grid = (BATCH, ATT_KV_HEADS, cpb)
        out_rows = BATCH * CTX_LEN
        out_spec = pl.BlockSpec((CHUNK, qw), lambda b, h, n: (b * cpb + n, h))
    return pl.pallas_call(
        functools.partial(_attn_kernel, window=window),
        grid=grid,
        in_specs=in_specs,
        out_specs=out_spec,
        out_shape=jax.ShapeDtypeStruct((out_rows, ATT_Q), bf16),
        compiler_params=_params("parallel", "parallel", "parallel"),
        name="attn_window" if window else "attn_context",
    )(*args)


CONV_COLS = 1024
HALO_ROWS = 16


def _seq_edges(i):
    nlat = BATCH * SEQ // CHUNK
    lat = i < nlat
    per = jnp.where(lat, SEQ // CHUNK, CTX_LEN // CHUNK)
    j = jnp.where(lat, i, i - nlat) % per
    return j == 0, j == per - 1


def _conv_kernel(x_ref, xp_ref, xn_ref, w_ref, b_ref, o_ref):
    i = pl.program_id(0)
    first, last = _seq_edges(i)
    x = x_ref[...].astype(f32)
    keep_p = jnp.where(first, 0.0, 1.0)
    keep_n = jnp.where(last, 0.0, 1.0)
    xp = xp_ref[...].astype(f32) * keep_p
    xn = xn_ref[...].astype(f32) * keep_n
    row = lax.broadcasted_iota(jnp.int32, x.shape, 0)
    pad = SSD_CONV // 2
    acc = b_ref[...] + x * w_ref[pad:pad + 1, :]
    for k in range(SSD_CONV):
        sh = pad - k
        if sh == 0:
            continue
        t = pltpu.roll(x, sh % CHUNK, 0)
        if sh > 0:
            for r in range(sh):
                t = jnp.where(row == r, xp[HALO_ROWS - sh + r:HALO_ROWS - sh + r + 1, :], t)
        else:
            for r in range(-sh):
                t = jnp.where(row == CHUNK + sh + r, xn[r:r + 1, :], t)
        acc = acc + t * w_ref[k:k + 1, :]
    o_ref[...] = _silu(acc).astype(bf16)


def _conv_silu(u_big, conv_w, conv_b):
    t_all = _n_tok()
    nblk = t_all // CHUNK
    hpc = CHUNK // HALO_ROWS
    c0 = C_XBC // CONV_COLS
    return pl.pallas_call(
        _conv_kernel,
        grid=(nblk, SSD_XBC // CONV_COLS),
        in_specs=[pl.BlockSpec((CHUNK, CONV_COLS), lambda i, j: (i, c0 + j)),
                  pl.BlockSpec((HALO_ROWS, CONV_COLS), lambda i, j: (jnp.maximum(i * hpc - 1, 0), c0 + j)),
                  pl.BlockSpec((HALO_ROWS, CONV_COLS),
                               lambda i, j: (jnp.minimum((i + 1) * hpc, nblk * hpc - 1), c0 + j)),
                  pl.BlockSpec((SSD_CONV, CONV_COLS), lambda i, j: (0, j)),
                  pl.BlockSpec((1, CONV_COLS), lambda i, j: (0, j))],
        out_specs=pl.BlockSpec((CHUNK, CONV_COLS), lambda i, j: (i, j)),
        out_shape=jax.ShapeDtypeStruct((t_all, SSD_XBC), bf16),
        compiler_params=_params("parallel", "parallel"),
        name="ssd_conv",
    )(u_big, u_big, u_big, conv_w, conv_b.reshape(1, SSD_XBC))


def _scan_block(b, d, c):
    ncc = CTX_LEN // CHUNK
    nlc = SEQ // CHUNK
    j_ctx = jnp.where(d == 0, c, ncc - 1 - c)
    j_lat = jnp.where(d == 0, c - ncc, ncc + nlc - 1 - c)
    return jnp.where(c < ncc, BATCH * nlc + b * ncc + j_ctx, b * nlc + j_lat)


def _dir_tri(d):
    s = lax.broadcasted_iota(jnp.int32, (CHUNK, CHUNK), 0)
    t = lax.broadcasted_iota(jnp.int32, (CHUNK, CHUNK), 1)
    return (t - s) * (1 - 2 * d) >= 0


def _dir_mask(d):
    t = lax.broadcasted_iota(jnp.int32, (CHUNK, CHUNK), 0)
    s = lax.broadcasted_iota(jnp.int32, (CHUNK, CHUNK), 1)
    return (t - s) * (1 - 2 * d) >= 0


def _ssd_kernel(xs_ref, sm_ref, dtb_ref, a_ref, e_ref, o_ref, st_ref):
    d = pl.program_id(1)
    c = pl.program_id(2)
    hpg = SSD_HEADS // SSD_GROUPS
    gw = hpg * SSD_HEAD_DIM

    @pl.when(c == 0)
    def _():
        st_ref[...] = jnp.zeros_like(st_ref)

    st = sm_ref[...].T
    raw = jnp.where(d == 0, st[S_DT:S_DT + SSD_HEADS], st[S_DT + SSD_HEADS:S_DT + 2 * SSD_HEADS])
    dt_t = _softplus(raw + dtb_ref[...])
    la_t = dt_t * a_ref[...]
    tri_b = _dir_tri(d)
    cum_t = _dot_exact(la_t, jnp.where(tri_b, 1.0, 0.0).astype(bf16))
    tot = jnp.where(d == 0, cum_t[:, CHUNK - 1:CHUNK], cum_t[:, 0:1])
    ecum_t = jnp.exp(cum_t)
    dtdec_t = dt_t * jnp.exp(tot - cum_t)
    z = jnp.concatenate([cum_t, dt_t, ecum_t, dtdec_t], axis=0)
    zt = z.T
    nh = SSD_HEADS
    expand = e_ref[...]
    dt_x = _dot_exact(zt[:, nh:2 * nh], expand)
    ecum_x = _dot_exact(zt[:, 2 * nh:3 * nh], expand)
    dtdec_x = _dot_exact(zt[:, 3 * nh:4 * nh], expand)
    tot_l = jnp.where(d == 0, zt[CHUNK - 1:CHUNK, 0:nh], zt[0:1, 0:nh])
    etot_x = _dot_exact(jnp.exp(tot_l), expand)

    xs = xs_ref[:, 0:SSD_INNER].astype(f32)
    xdt = (xs * dt_x).astype(bf16)
    xdd = (xs * dtdec_x).astype(bf16)
    mask_ts = _dir_mask(d)
    lane = lax.broadcasted_iota(jnp.int32, (CHUNK, 2 * SSD_HEAD_DIM), 1)
    for g in range(SSD_GROUPS):
        bm = xs_ref[:, SSD_INNER + g * SSD_STATE:SSD_INNER + (g + 1) * SSD_STATE]
        cm = xs_ref[:, SSD_INNER + SSD_BC + g * SSD_STATE:SSD_INNER + SSD_BC + (g + 1) * SSD_STATE]
        cb = _dot_nt(cm, bm)
        h_in = st_ref[g]
        y_off = _dot(cm, h_in.astype(bf16)) * ecum_x[:, g * gw:(g + 1) * gw]
        for pr in range(hpg // 2):
            lo = g * gw + pr * 2 * SSD_HEAD_DIM
            xp = xdt[:, lo:lo + 2 * SSD_HEAD_DIM]
            ys = []
            for r in range(2):
                hh = g * hpg + pr * 2 + r
                seg = zt[:, hh:hh + 1] - cum_t[hh:hh + 1, :]
                lm = jnp.exp(jnp.where(mask_ts, seg, NEG_BIG))
                ys.append(_dot((cb * lm).astype(bf16), xp))
            y_pair = jnp.where(lane < SSD_HEAD_DIM, ys[0], ys[1])
            o_ref[:, lo:lo + 2 * SSD_HEAD_DIM] = y_pair + y_off[:, pr * 2 * SSD_HEAD_DIM:(pr + 1) * 2 * SSD_HEAD_DIM]
        upd = _dot_tn(bm, xdd[:, g * gw:(g + 1) * gw])
        st_ref[g] = h_in * etot_x[:, g * gw:(g + 1) * gw] + upd


def _ssd_scan(xs_act, u_small, dt_bias, a_log, expand):
    t_all = _n_tok()
    steps = (CTX_LEN + SEQ) // CHUNK
    hpg = SSD_HEADS // SSD_GROUPS
    a_neg = -jnp.exp(a_log.astype(f32))
    return pl.pallas_call(
        _ssd_kernel,
        grid=(BATCH, 2, steps),
        in_specs=[pl.BlockSpec((CHUNK, SSD_XBC), lambda b, d, c: (_scan_block(b, d, c), 0)),
                  pl.BlockSpec((CHUNK, LANES), lambda b, d, c: (_scan_block(b, d, c), 0)),
                  pl.BlockSpec((None, SSD_HEADS, 1), lambda b, d, c: (d, 0, 0)),
                  pl.BlockSpec((None, SSD_HEADS, 1), lambda b, d, c: (d, 0, 0)),
                  pl.BlockSpec((SSD_HEADS, SSD_INNER), lambda b, d, c: (0, 0))],
        out_specs=pl.BlockSpec((None, CHUNK, SSD_INNER), lambda b, d, c: (d, _scan_block(b, d, c), 0)),
        out_shape=jax.ShapeDtypeStruct((2, t_all, SSD_INNER), f32),
        scratch_shapes=[pltpu.VMEM((SSD_GROUPS, SSD_STATE, hpg * SSD_HEAD_DIM), f32)],
        compiler_params=_params("parallel", "parallel", "arbitrary"),
        name="ssd_scan",
    )(xs_act, u_small, dt_bias.astype(f32)[..., None], a_neg[..., None], expand)


def _ssd_fin_kernel(y_ref, x0_ref, x1_ref, z0_ref, z1_ref, d_ref, g_ref, o_ref):
    xs = jnp.concatenate([x0_ref[...], x1_ref[...]], axis=1).astype(f32)
    z = jnp.concatenate([z0_ref[...], z1_ref[...]], axis=1).astype(f32)
    y = (y_ref[0] + y_ref[1] + xs * d_ref[...]) * _silu(z)
    y = y * lax.rsqrt(jnp.mean(y * y, axis=-1, keepdims=True) + NORM_EPS) * g_ref[...]
    o_ref[...] = y.astype(bf16)


def _ssd_finish(y_dirs, xs_act, u_big, d_skip, norm_g, n_rows):
    tm = _pick(n_rows, (256, 128))
    half = SSD_INNER // 2
    zc = C_Z // half
    d_x = jnp.repeat(d_skip.astype(f32), SSD_HEAD_DIM).reshape(1, SSD_INNER)
    return pl.pallas_call(
        _ssd_fin_kernel,
        grid=(n_rows // tm,),
        in_specs=[pl.BlockSpec((2, tm, SSD_INNER), lambda i: (0, i, 0)),
                  pl.BlockSpec((tm, half), lambda i: (i, 0)),
                  pl.BlockSpec((tm, half), lambda i: (i, 1)),
                  pl.BlockSpec((tm, half), lambda i: (i, zc)),
                  pl.BlockSpec((tm, half), lambda i: (i, zc + 1)),
                  pl.BlockSpec((1, SSD_INNER), lambda i: (0, 0)),
                  pl.BlockSpec((1, SSD_INNER), lambda i: (0, 0))],
        out_specs=pl.BlockSpec((tm, SSD_INNER), lambda i: (i, 0)),
        out_shape=jax.ShapeDtypeStruct((n_rows, SSD_INNER), bf16),
        compiler_params=_params("parallel"),
        name="ssd_finish",
    )(y_dirs, xs_act, xs_act, u_big, u_big, d_x, norm_g.reshape(1, SSD_INNER))


def _ml_kernel(q_ref, k_ref, v_ref, sm_ref, ib_ref, fb_ref, o_ref, s_ref, n_ref, m_ref):
    d = pl.program_id(1)
    c = pl.program_id(2)
    nh = ML_HEADS

    @pl.when(c == 0)
    def _():
        s_ref[...] = jnp.zeros_like(s_ref)
        n_ref[...] = jnp.zeros_like(n_ref)
        m_ref[...] = jnp.zeros_like(m_ref)

    st = sm_ref[...].T
    li_t = jnp.where(d == 0, st[S_MLI:S_MLI + nh], st[S_MLI + nh:S_MLI + 2 * nh]) + ib_ref[...]
    uf_t = jnp.where(d == 0, st[S_MLF:S_MLF + nh], st[S_MLF + nh:S_MLF + 2 * nh]) + fb_ref[...]
    lf_t = jnp.minimum(uf_t, 0.0) - jnp.log1p(jnp.exp(-jnp.abs(uf_t)))
    tri_b = _dir_tri(d)
    cum_t = _dot_exact(lf_t, jnp.where(tri_b, 1.0, 0.0).astype(bf16))
    g_tot = jnp.where(d == 0, cum_t[:, CHUNK - 1:CHUNK], cum_t[:, 0:1])
    w_end = g_tot - cum_t + li_t
    m_loc = jnp.max(w_end, axis=1, keepdims=True)
    e_end_t = jnp.exp(w_end - m_loc)
    m_prev = m_ref[:, 0:1]
    inter_t = cum_t + m_prev
    m_new = jnp.maximum(g_tot + m_prev, m_loc)
    wa = jnp.exp(g_tot + m_prev - m_new)
    wb = jnp.exp(m_loc - m_new)
    src_t = li_t - cum_t
    pad = jnp.zeros((LANES - 4 * nh, CHUNK), f32)
    z = jnp.concatenate([cum_t, e_end_t, inter_t, src_t, pad], axis=0)
    zt = z.T
    mask_ts = _dir_mask(d)
    k_scale = ML_QK_DIM ** -0.5
    for h in range(nh):
        q = q_ref[:, h * ML_QK_DIM:(h + 1) * ML_QK_DIM]
        kk = k_ref[:, h * ML_QK_DIM:(h + 1) * ML_QK_DIM]
        v = v_ref[:, h * ML_V_DIM:(h + 1) * ML_V_DIM]
        qk = _dot_nt(q, kk) * k_scale
        dm = jnp.where(mask_ts, zt[:, h:h + 1] + src_t[h:h + 1, :], NEG_BIG)
        inter_c = zt[:, 2 * nh + h:2 * nh + h + 1]
        m_t = jnp.maximum(inter_c, jnp.max(dm, axis=1, keepdims=True))
        pm = jnp.exp(dm - m_t) * qk
        e_int = jnp.exp(inter_c - m_t)
        s_in = s_ref[h]
        n_in = n_ref[h:h + 1, :]
        num = _dot(pm.astype(bf16), v) + e_int * _dot(q, s_in.astype(bf16))
        den = jnp.sum(pm, axis=1, keepdims=True) + e_int * jnp.sum(q.astype(f32) * n_in, axis=1, keepdims=True)
        o_ref[:, h * ML_V_DIM:(h + 1) * ML_V_DIM] = num / jnp.maximum(jnp.abs(den), jnp.exp(-m_t))
        ke = kk.astype(f32) * (k_scale * zt[:, nh + h:nh + h + 1])
        s_ref[h] = wa[h:h + 1, :] * s_in + wb[h:h + 1, :] * _dot_tn(ke.astype(bf16), v)
        n_ref[h:h + 1, :] = wa[h:h + 1, :] * n_in + wb[h:h + 1, :] * jnp.sum(ke, axis=0, keepdims=True)
    m_ref[...] = jnp.broadcast_to(m_new, m_ref.shape)


def _ml_scan(u_big, u_small, i_bias, f_bias):
    t_all = _n_tok()
    steps = (CTX_LEN + SEQ) // CHUNK
    blk = lambda b, d, c: _scan_block(b, d, c)
    return pl.pallas_call(
        _ml_kernel,
        grid=(BATCH, 2, steps),
        in_specs=[pl.BlockSpec((CHUNK, ML_QK), lambda b, d, c: (blk(b, d, c), C_MLQ // ML_QK)),
                  pl.BlockSpec((CHUNK, ML_QK), lambda b, d, c: (blk(b, d, c), C_MLK // ML_QK)),
                  pl.BlockSpec((CHUNK, ML_V), lambda b, d, c: (blk(b, d, c), C_MLV // ML_V)),
                  pl.BlockSpec((CHUNK, LANES), lambda b, d, c: (blk(b, d, c), 0)),
                  pl.BlockSpec((None, ML_HEADS, 1), lambda b, d, c: (d, 0, 0)),
                  pl.BlockSpec((None, ML_HEADS, 1), lambda b, d, c: (d, 0, 0))],
        out_specs=pl.BlockSpec((None, CHUNK, ML_V), lambda b, d, c: (d, blk(b, d, c), 0)),
        out_shape=jax.ShapeDtypeStruct((2, t_all, ML_V), f32),
        scratch_shapes=[pltpu.VMEM((ML_HEADS, ML_QK_DIM, ML_V_DIM), f32),
                        pltpu.VMEM((ML_HEADS, ML_QK_DIM), f32),
                        pltpu.VMEM((ML_HEADS, LANES), f32)],
        compiler_params=_params("parallel", "parallel", "arbitrary"),
        name="mlstm_scan",
    )(u_big, u_big, u_big, u_small, i_bias.astype(f32)[..., None], f_bias.astype(f32)[..., None])


def _ml_fin_kernel(h_ref, o_gate_ref, g_ref, o_ref):
    for h in range(ML_HEADS):
        sl = slice(h * ML_V_DIM, (h + 1) * ML_V_DIM)
        t = h_ref[0, :, sl] + h_ref[1, :, sl]
        t = t * lax.rsqrt(jnp.mean(t * t, axis=-1, keepdims=True) + NORM_EPS) * g_ref[:, sl]
        o_ref[:, sl] = (t * _sigmoid(o_gate_ref[:, sl].astype(f32))).astype(bf16)


def _ml_finish(h_dirs, u_big, norm_g, n_rows):
    tm = _pick(n_rows, (256, 128))
    return pl.pallas_call(
        _ml_fin_kernel,
        grid=(n_rows // tm,),
        in_specs=[pl.BlockSpec((2, tm, ML_V), lambda i: (0, i, 0)),
                  pl.BlockSpec((tm, ML_V), lambda i: (i, C_MLO // ML_V)),
                  pl.BlockSpec((1, ML_V), lambda i: (0, 0))],
        out_specs=pl.BlockSpec((tm, ML_V), lambda i: (i, 0)),
        out_shape=jax.ShapeDtypeStruct((n_rows, ML_V), bf16),
        compiler_params=_params("parallel"),
        name="mlstm_finish",
    )(h_dirs, u_big, norm_g.reshape(1, ML_V))


def _merge_kernel(ya_ref, ys_ref, ym_ref, wa_ref, ws_ref, wm_ref, ga_ref, gs_ref, gm_ref, o_ref):
    m = (_sigmoid(ga_ref[...].astype(f32)) * _dot(ya_ref[...], wa_ref[...])
         + _sigmoid(gs_ref[...].astype(f32)) * _dot(ys_ref[...], ws_ref[...])
         + _sigmoid(gm_ref[...].astype(f32)) * _dot(ym_ref[...], wm_ref[...]))
    o_ref[...] = m.astype(bf16)


def _merge(y_att, y_ssd, y_ml, w_att, w_ssd, w_ml, u_big, n_rows):
    tm = _pick(n_rows, (512, 256, 128))
    tn = _pick(D_MODEL, (512, 256, 128))
    g0 = C_GATE // tn
    gpb = D_MODEL // tn
    act = lambda w: pl.BlockSpec((tm, w), lambda i, j: (i, 0))
    wgt = lambda w: pl.BlockSpec((w, tn), lambda i, j: (0, j))
    gate = lambda k: pl.BlockSpec((tm, tn), lambda i, j: (i, g0 + k * gpb + j))
    return pl.pallas_call(
        _merge_kernel,
        grid=(n_rows // tm, D_MODEL // tn),
        in_specs=[act(ATT_Q), act(SSD_INNER), act(ML_V), wgt(ATT_Q), wgt(SSD_INNER), wgt(ML_V),
                  gate(0), gate(1), gate(2)],
        out_specs=pl.BlockSpec((tm, tn), lambda i, j: (i, j)),
        out_shape=jax.ShapeDtypeStruct((n_rows, D_MODEL), bf16),
        compiler_params=_params("parallel", "parallel"),
        name="branch_merge",
    )(y_att, y_ssd, y_ml, w_att, w_ssd, w_ml, u_big, u_big, u_big)


def _outproj_kernel(m_ref, w_ref, x_ref, g_ref, o_ref):
    o_ref[...] = x_ref[...] + g_ref[...] * _dot(m_ref[...], w_ref[...])


def _outproj(m, w_out, tokens, mod6, n_rows):
    tm = _pick(math.gcd(n_rows, SEQ), (512, 256, 128))
    tn = _pick(D_MODEL, (512, 256, 128))
    return pl.pallas_call(
        _outproj_kernel,
        grid=(n_rows // tm, D_MODEL // tn),
        in_specs=[pl.BlockSpec((tm, D_MODEL), lambda i, j: (i, 0)),
                  pl.BlockSpec((D_MODEL, tn), lambda i, j: (0, j)),
                  pl.BlockSpec((tm, tn), lambda i, j: (i, j)),
                  pl.BlockSpec((None, None, 1, tn), lambda i, j: (_mod_row(i * tm), 2, 0, j))],
        out_specs=pl.BlockSpec((tm, tn), lambda i, j: (i, j)),
        out_shape=jax.ShapeDtypeStruct((n_rows, D_MODEL), f32),
        compiler_params=_params("parallel", "parallel"),
        name="out_proj",
    )(m, w_out, tokens, mod6)


ROUTE_TILE = 512


def _route_kernel(lg_ref, rb_ref, e_ref, w_ref, r_ref, cnt_ref, base_ref):
    i = pl.program_id(0)

    @pl.when(i == 0)
    def _():
        base_ref[...] = jnp.zeros_like(base_ref)

    s = _sigmoid(lg_ref[...])
    sel = s + rb_ref[...]
    tm = s.shape[1]
    epg = EXPERTS_PER_GROUP
    io = lax.broadcasted_iota(jnp.int32, (epg, tm), 0)
    best = None
    for g in range(N_EXPERT_GROUPS):
        x = sel[g * epg:(g + 1) * epg]
        sg = s[g * epg:(g + 1) * epg]
        m1 = jnp.max(x, axis=0, keepdims=True)
        i1 = jnp.min(jnp.where(x == m1, io, epg), axis=0, keepdims=True)
        x2 = jnp.where(io == i1, -jnp.inf, x)
        m2 = jnp.max(x2, axis=0, keepdims=True)
        i2 = jnp.min(jnp.where(x2 == m2, io, epg), axis=0, keepdims=True)
        w1 = jnp.sum(jnp.where(io == i1, sg, 0.0), axis=0, keepdims=True)
        w2 = jnp.sum(jnp.where(io == i2, sg, 0.0), axis=0, keepdims=True)
        cand = (m1 + m2, i1 + g * epg, i2 + g * epg, w1, w2)
        if best is None:
            best = cand
        else:
            take = cand[0] > best[0]
            best = tuple(jnp.where(take, a, b) for a, b in zip(cand, best))
    _, e1, e2, w1, w2 = best
    wsum = w1 + w2
    e_ref[...] = jnp.concatenate([e1, e2], axis=0)
    w_ref[...] = jnp.concatenate([w1 / wsum, w2 / wsum], axis=0)
    ioe = lax.broadcasted_iota(jnp.int32, (N_EXPERTS, tm), 0)
    oh1 = ioe == e1
    oh2 = ioe == e2
    oh = jnp.where(oh1 | oh2, 1.0, 0.0)
    before = lax.broadcasted_iota(jnp.int32, (tm, tm), 0) < lax.broadcasted_iota(jnp.int32, (tm, tm), 1)
    pos = base_ref[:, 0:1] + _dot(oh.astype(bf16), jnp.where(before, 1.0, 0.0).astype(bf16))
    r1 = jnp.sum(jnp.where(oh1, pos, 0.0), axis=0, keepdims=True)
    r2 = jnp.sum(jnp.where(oh2, pos, 0.0), axis=0, keepdims=True)
    r_ref[...] = jnp.concatenate([r1, r2], axis=0).astype(jnp.int32)
    base_ref[...] = base_ref[...] + jnp.sum(oh, axis=1, keepdims=True)
    cnt_ref[...] = base_ref[...]


def _route(logits_t, router_b, n_rows):
    tm = _pick(n_rows, (ROUTE_TILE, 256, 128))
    pair = lambda dt: jax.ShapeDtypeStruct((TOP_K, n_rows), dt)
    pspec = pl.BlockSpec((TOP_K, tm), lambda i: (0, i))
    return pl.pallas_call(
        _route_kernel,
        grid=(n_rows // tm,),
        in_specs=[pl.BlockSpec((N_EXPERTS, tm), lambda i: (0, i)),
                  pl.BlockSpec((N_EXPERTS, 1), lambda i: (0, 0))],
        out_specs=(pspec, pspec, pspec, pl.BlockSpec((N_EXPERTS, LANES), lambda i: (0, 0))),
        out_shape=(pair(jnp.int32), pair(f32), pair(jnp.int32),
                   jax.ShapeDtypeStruct((N_EXPERTS, LANES), f32)),
        scratch_shapes=[pltpu.VMEM((N_EXPERTS, LANES), f32)],
        compiler_params=_params("arbitrary"),
        name="moe_route",
    )(logits_t, router_b.astype(f32).reshape(N_EXPERTS, 1))


def _ffn_kernel(be_ref, nu_ref, x_ref, wg_ref, wu_ref, wd_ref, o_ref):
    i = pl.program_id(0)

    @pl.when(i < nu_ref[0])
    def _():
        x = x_ref[...]
        a = _silu(_dot(x, wg_ref[...])) * _dot(x, wu_ref[...])
        o_ref[...] = _dot(a.astype(bf16), wd_ref[...]).astype(o_ref.dtype)

    @pl.when(i >= nu_ref[0])
    def _():
        o_ref[...] = jnp.zeros_like(o_ref)


def _expert_ffn(buf, block_e, n_used, w_gate, w_up, w_down):
    n_blocks = buf.shape[0] // MOE_ROWS
    return pl.pallas_call(
        _ffn_kernel,
        grid_spec=pltpu.PrefetchScalarGridSpec(
            num_scalar_prefetch=2,
            grid=(n_blocks,),
            in_specs=[pl.BlockSpec((MOE_ROWS, D_MODEL), lambda i, be, nu: (i, 0)),
                      pl.BlockSpec((None, D_MODEL, EXPERT_FF), lambda i, be, nu: (be[i], 0, 0)),
                      pl.BlockSpec((None, D_MODEL, EXPERT_FF), lambda i, be, nu: (be[i], 0, 0)),
                      pl.BlockSpec((None, EXPERT_FF, D_MODEL), lambda i, be, nu: (be[i], 0, 0))],
            out_specs=pl.BlockSpec((MOE_ROWS, D_MODEL), lambda i, be, nu: (i, 0))),
        out_shape=jax.ShapeDtypeStruct(buf.shape, bf16),
        compiler_params=_params("arbitrary"),
        name="moe_experts",
    )(block_e, n_used, buf, w_gate, w_up, w_down)


def _combine_kernel(x_ref, y1_ref, y2_ref, w1_ref, w2_ref, g_ref, o_ref):
    y = w1_ref[...] * y1_ref[...].astype(f32) + w2_ref[...] * y2_ref[...].astype(f32)
    o_ref[...] = x_ref[...] + g_ref[...] * y


def _combine(tokens, y1, y2, w1, w2, mod6, n_rows):
    tm = _pick(math.gcd(n_rows, SEQ), (256, 128))
    row = pl.BlockSpec((tm, D_MODEL), lambda i: (i, 0))
    col = pl.BlockSpec((tm, 1), lambda i: (i, 0))
    return pl.pallas_call(
        _combine_kernel,
        grid=(n_rows // tm,),
        in_specs=[row, row, row, col, col,
                  pl.BlockSpec((None, None, 1, D_MODEL), lambda i: (_mod_row(i * tm), 5, 0, 0))],
        out_specs=row,
        out_shape=jax.ShapeDtypeStruct((n_rows, D_MODEL), f32),
        compiler_params=_params("parallel"),
        name="moe_combine",
    )(tokens, y1, y2, w1, w2, mod6)


def _moe(tokens, n_rows, gain, mod6, router_wt, router_b, w_gate, w_up, w_down):
    hb, logits_t = _norm_mod(tokens, n_rows, gain, mod6, 3, 4, router_wt)
    e_idx, gate, rank, counts = _route(logits_t, router_b, n_rows)
    sizes = counts[:, 0].astype(jnp.int32)
    padded = (sizes + MOE_ROWS - 1) // MOE_ROWS * MOE_ROWS
    pad_end = jnp.cumsum(padded)
    pad_start = pad_end - padded
    dest = pad_start[e_idx] + rank
    n_assign = n_rows * TOP_K
    n_blocks = (n_assign + N_EXPERTS * (MOE_ROWS - 1) + MOE_ROWS - 1) // MOE_ROWS
    block_e = jnp.minimum(jnp.searchsorted(pad_end, jnp.arange(n_blocks, dtype=jnp.int32) * MOE_ROWS, side='right'),
                          N_EXPERTS - 1).astype(jnp.int32)
    n_used = (pad_end[-1] // MOE_ROWS).astype(jnp.int32).reshape(1)
    tok = jnp.broadcast_to(jnp.arange(n_rows, dtype=jnp.int32), (TOP_K, n_rows))
    src = jnp.zeros((n_blocks * MOE_ROWS,), jnp.int32).at[dest.reshape(-1)].set(tok.reshape(-1))
    buf = jnp.take(hb, src, axis=0)
    ys = _expert_ffn(buf, block_e, n_used, w_gate, w_up, w_down)
    y1 = jnp.take(ys, dest[0], axis=0)
    y2 = jnp.take(ys, dest[1], axis=0)
    return _combine(tokens, y1, y2, gate[0][:, None], gate[1][:, None], mod6, n_rows)


def _rope_tables():
    rows = SEQ // GRID_W
    freqs = ROPE_BASE ** (-jnp.arange(ROPE_PAIRS_PER_AXIS, dtype=f32) / ROPE_PAIRS_PER_AXIS)
    shape = (rows, GRID_W, ROPE_PAIRS_PER_AXIS)
    ang_r = jnp.broadcast_to(jnp.arange(rows, dtype=f32)[:, None, None] * freqs, shape)
    ang_c = jnp.broadcast_to(jnp.arange(GRID_W, dtype=f32)[None, :, None] * freqs, shape)
    ang = jnp.concatenate([ang_r, ang_c], axis=-1).reshape(SEQ, 2 * ROPE_PAIRS_PER_AXIS)
    cos, sin = jnp.cos(ang), jnp.sin(ang)
    cos_t = jnp.concatenate([cos, cos], axis=-1)
    sin_t = jnp.concatenate([-sin, sin], axis=-1)
    cos_t = jnp.concatenate([cos_t, jnp.ones((CHUNK, HEAD_DIM), f32)], axis=0)
    sin_t = jnp.concatenate([sin_t, jnp.zeros((CHUNK, HEAD_DIM), f32)], axis=0)
    return cos_t, sin_t


def _repack_w_in(w):
    big = jnp.concatenate([w[:, :_OFF_DT], w[:, _OFF_MLQ:_OFF_MLI], w[:, _OFF_GATE:]], axis=1).astype(bf16)
    n_small = 2 * SSD_HEADS + 4 * ML_HEADS
    small = jnp.concatenate([w[:, _OFF_DT:_OFF_MLQ], w[:, _OFF_MLI:_OFF_GATE],
                             jnp.zeros((D_MODEL, LANES - n_small), w.dtype)], axis=1).astype(bf16)
    return big, small


def kernel(x, c, ctx, c_ctx, norm1_g, norm2_g, w_mod, b_mod, w_in, q_norm_g, k_norm_g, attn_sink,
           ssd_conv_w, ssd_conv_b, ssd_dt_bias, ssd_a_log, ssd_d, ssd_norm_g,
           ml_i_bias, ml_f_bias, ml_norm_g, w_br_att, w_br_ssd, w_br_ml, w_out,
           router_w, router_b, w_gate, w_up, w_down):
    n_lat = BATCH * SEQ
    t_all = _n_tok()
    tokens = jnp.concatenate([x.reshape(n_lat, D_MODEL), ctx.reshape(BATCH * CTX_LEN, D_MODEL)], axis=0)
    cond8 = jnp.concatenate([c, c_ctx[None, :], jnp.zeros((8 - BATCH - 1, D_MODEL), f32)], axis=0)
    cos_t, sin_t = _rope_tables()
    router_wt = router_w.astype(f32).T
    expand = (jnp.arange(SSD_INNER)[None, :] // SSD_HEAD_DIM == jnp.arange(SSD_HEADS)[:, None]).astype(bf16)

    for l in range(DEPTH):
        ctx_out = l < DEPTH - 1
        n_out = t_all if ctx_out else n_lat
        mod6 = _modulation(cond8, w_mod[l], b_mod[l]).reshape(8, N_MOD, 1, D_MODEL)
        a = _norm_mod(tokens, t_all, norm1_g[l], mod6, 0, 1)
        w_big, w_small = _repack_w_in(w_in[l])
        u_big = _matmul(a, w_big, bf16, "in_proj")
        u_small = _matmul(a, w_small, f32, "in_proj_small")

        k_rot = _kprep(u_big, cos_t, sin_t, k_norm_g[l])
        y_att = _attention(u_big, k_rot, cos_t, sin_t, q_norm_g[l], attn_sink[l].astype(f32), True)
        if ctx_out:
            y_att_c = _attention(u_big, k_rot, cos_t, sin_t, q_norm_g[l], attn_sink[l].astype(f32), False)
            y_att = jnp.concatenate([y_att, y_att_c], axis=0)

        xs_act = _conv_silu(u_big, ssd_conv_w[l].astype(f32), ssd_conv_b[l].astype(f32))
        y_dirs = _ssd_scan(xs_act, u_small, ssd_dt_bias[l], ssd_a_log[l], expand)
        y_ssd = _ssd_finish(y_dirs, xs_act, u_big, ssd_d[l], ssd_norm_g[l], n_out)

        h_dirs = _ml_scan(u_big, u_small, ml_i_bias[l], ml_f_bias[l])
        y_ml = _ml_finish(h_dirs, u_big, ml_norm_g[l], n_out)

        m = _merge(y_att, y_ssd, y_ml, w_br_att[l].astype(bf16), w_br_ssd[l].astype(bf16),
                   w_br_ml[l].astype(bf16), u_big, n_out)
        tokens = _outproj(m, w_out[l].astype(bf16), tokens, mod6, n_out)
        tokens = _moe(tokens, n_out, norm2_g[l], mod6, router_wt, router_b,
                      w_gate[l].astype(bf16), w_up[l].astype(bf16), w_down[l].astype(bf16))
    return tokens[:n_lat].reshape(BATCH, SEQ, D_MODEL)
```
